```python
import jax, jax.numpy as jnp
from jax import lax
import numpy as np

D_MODEL = 2048
BATCH = 2
SEQ = 4096
DEPTH = 1

GDN_HEADS = 8
GDN_HEAD_DIM = 128
GDN_WIDTH = GDN_HEADS * GDN_HEAD_DIM
GDN_CONV = 4
GDN_CHUNK = 64
NSA_HEADS = 8
NSA_KV_HEADS = 2
NSA_HEAD_DIM = 128
NSA_WIDTH = NSA_HEADS * NSA_HEAD_DIM
NSA_KV_WIDTH = NSA_KV_HEADS * NSA_HEAD_DIM
CMP_BLOCK = 32
CMP_STRIDE = 16
SLC_BLOCK = 64
SLC_TOPK = 16
WINDOW = 512
Q_BLOCK = 128
ROPE_THETA = 500000.0
ROPE_DIM = NSA_HEAD_DIM // 4
D_FF = 5632
N_MOD = 9
N_IN = 4 * GDN_WIDTH + 2 * GDN_HEADS + NSA_WIDTH + 6 * NSA_KV_WIDTH + 3 * NSA_HEADS + 2 * D_MODEL
EPS = 1e-6
NEG_INF = -1e30
FORCE_BONUS = 1e3

kernel_name = "hybrid_gdn_nsa_macaron_adaln"


def _rms_norm(x, w):
    xf = x.astype(jnp.float32)
    y = xf * lax.rsqrt(jnp.mean(xf * xf, axis=-1, keepdims=True) + EPS)
    return (y * w.astype(jnp.float32)).astype(x.dtype)


def _modulate(x, gain, shift, scale):
    return _rms_norm(x, gain) * (1.0 + scale[:, None, :]) + shift[:, None, :]


def _swiglu(h, w_gate, w_up, w_down):
    return (jax.nn.silu(h @ w_gate) * (h @ w_up)) @ w_down


def _l2norm(x):
    return x * lax.rsqrt(jnp.sum(x * x, axis=-1, keepdims=True) + EPS)


def _masked_softmax(s, mask):
    s = jnp.where(mask, s.astype(jnp.float32), NEG_INF)
    m = jnp.max(s, axis=-1, keepdims=True)
    e = jnp.where(mask, jnp.exp(s - m), 0.0)
    den = jnp.sum(e, axis=-1, keepdims=True)
    return e / jnp.where(den > 0, den, 1.0)


def _partial_rope(x, cos, sin):
    half = ROPE_DIM // 2
    c = cos[None, :, None, :].astype(x.dtype)
    s = sin[None, :, None, :].astype(x.dtype)
    x1, x2, xp = x[..., :half], x[..., half:ROPE_DIM], x[..., ROPE_DIM:]
    return jnp.concatenate([x1 * c - x2 * s, x2 * c + x1 * s, xp], axis=-1)


def _causal_conv(x, w):
    k = w.shape[1]
    return lax.conv_general_dilated(
        x, w.T[:, None, :].astype(x.dtype), window_strides=(1,), padding=[(k - 1, 0)],
        dimension_numbers=('NWC', 'WIO', 'NWC'), feature_group_count=x.shape[-1])


def _gated_delta_rule(q, k, v, g, beta):
    b, s, h, dk = q.shape
    c = GDN_CHUNK
    n = s // c
    q = _l2norm(q) * (dk ** -0.5)
    k = _l2norm(k)

    def chunks(t):
        return t.reshape(b, n, c, h, -1).transpose(0, 3, 1, 2, 4)

    q, k, v = chunks(q), chunks(k), chunks(v)
    g = g.reshape(b, n, c, h).transpose(0, 3, 1, 2)
    beta = beta.reshape(b, n, c, h).transpose(0, 3, 1, 2)
    gc = jnp.cumsum(g, axis=-1)
    causal = jnp.tril(jnp.ones((c, c), bool))
    strict = jnp.tril(jnp.ones((c, c), bool), -1)
    decay = jnp.exp(jnp.where(causal, gc[..., :, None] - gc[..., None, :], -jnp.inf))
    k_beta = k * beta[..., None]
    lmat = jnp.where(strict, jnp.einsum('bhnid,bhnjd->bhnij', k_beta, k) * decay, 0.0)
    eye = jnp.eye(c, dtype=jnp.float32)
    tmat = lax.linalg.triangular_solve(eye + lmat, jnp.broadcast_to(eye, lmat.shape),
                                       left_side=True, lower=True, unit_diagonal=True)
    u = tmat @ (v * beta[..., None])
    w = tmat @ (k_beta * jnp.exp(gc)[..., None])
    a_intra = jnp.where(causal, jnp.einsum('bhnid,bhnjd->bhnij', q, k) * decay, 0.0)
    q_g = q * jnp.exp(gc)[..., None]
    k_d = k * jnp.exp(gc[..., -1:] - gc)[..., None]
    d_end = jnp.exp(gc[..., -1])

    def step(state, inp):
        qg_i, kd_i, u_i, w_i, a_i, de_i = inp
        v_new = u_i - w_i @ state
        o_i = qg_i @ state + a_i @ v_new
        state = state * de_i[..., None, None] + jnp.swapaxes(kd_i, -1, -2) @ v_new
        return state, o_i

    xs = tuple(jnp.moveaxis(t, 2, 0) for t in (q_g, k_d, u, w, a_intra, d_end))
    state0 = jnp.zeros((b, h, dk, v.shape[-1]), jnp.float32)
    _, o = lax.scan(step, state0, xs)
    return o.transpose(1, 0, 3, 2, 4).reshape(b, s, h, -1)


def _compress(x, pos, w1, b1, w2):
    b, s, g, d = x.shape
    r = CMP_BLOCK // CMP_STRIDE
    n_cmp = s // CMP_STRIDE - r + 1
    xs = x.reshape(b, s // CMP_STRIDE, CMP_STRIDE, g, d)
    blocks = jnp.concatenate([xs[:, j:j + n_cmp] for j in range(r)], axis=2)
    blocks = blocks + pos[None, None, :, None, :]
    flat = blocks.transpose(0, 1, 3, 2, 4).reshape(b, n_cmp, g, CMP_BLOCK * d)
    return jax.nn.silu(flat @ w1 + b1) @ w2


def _nsa(q, qr, kcmp, vcmp, ks, vs, kw, vw, gates):
    b, s, hq, d = q.shape
    g = ks.shape[2]
    r = hq // g
    scale = d ** -0.5
    q5 = q.reshape(b, s, g, r, d)
    qr5 = qr.reshape(b, s, g, r, d)
    n_cmp = kcmp.shape[1]
    cmp_end = jnp.arange(n_cmp) * CMP_STRIDE + (CMP_BLOCK - 1)
    n_slc = s // SLC_BLOCK
    topk = min(SLC_TOPK, n_slc)
    kb = ks.reshape(b, n_slc, SLC_BLOCK, g, d).transpose(0, 3, 1, 2, 4)
    vb = vs.reshape(b, n_slc, SLC_BLOCK, g, d).transpose(0, 3, 1, 2, 4)
    kw_pad = jnp.pad(kw, ((0, 0), (WINDOW, 0), (0, 0), (0, 0)))
    vw_pad = jnp.pad(vw, ((0, 0), (WINDOW, 0), (0, 0), (0, 0)))
    b_idx = jnp.arange(b)[:, None, None, None]
    g_idx = jnp.arange(g)[None, :, None, None]
    blk = jnp.arange(n_slc)

    def one_block(qb):
        q0 = qb * Q_BLOCK
        t = q0 + jnp.arange(Q_BLOCK)
        qc = lax.dynamic_slice_in_dim(q5, q0, Q_BLOCK, 1)
        qs = lax.dynamic_slice_in_dim(qr5, q0, Q_BLOCK, 1)
        gt = lax.dynamic_slice_in_dim(gates, q0, Q_BLOCK, 1)
        s_c = jnp.einsum('bqgrd,bngd->bgrqn', qc, kcmp) * scale
        p_c = _masked_softmax(s_c, cmp_end[None, :] <= t[:, None])
        o_c = jnp.einsum('bgrqn,bngd->bqgrd', p_c.astype(vcmp.dtype), vcmp)
        imp = jnp.sum(p_c, axis=2)
        imp = jnp.pad(imp, ((0, 0), (0, 0), (0, 0), (0, s // CMP_STRIDE - n_cmp)))
        imp = imp.reshape(b, g, Q_BLOCK, n_slc, SLC_BLOCK // CMP_STRIDE).sum(-1)
        tb = t // SLC_BLOCK
        visible = blk[None, :] * SLC_BLOCK <= t[:, None]
        forced = (blk[None, :] == 0) | (blk[None, :] == tb[:, None]) | (blk[None, :] == tb[:, None] - 1)
        score = jnp.where(visible, imp + jnp.where(forced, FORCE_BONUS, 0.0), NEG_INF)
        top_val, top_idx = lax.top_k(score, topk)
        k_sel = kb[b_idx, g_idx, top_idx]
        v_sel = vb[b_idx, g_idx, top_idx]
        tok = top_idx[..., None] * SLC_BLOCK + jnp.arange(SLC_BLOCK)
        sel_mask = (top_val > 0.5 * NEG_INF)[..., None] & (tok <= t[None, None, :, None, None])
        s_s = jnp.einsum('bqgrd,bgqjkd->bgrqjk', qs, k_sel) * scale
        s_s = s_s.reshape(b, g, r, Q_BLOCK, topk * SLC_BLOCK)
        p_s = _masked_softmax(s_s, sel_mask.reshape(b, g, 1, Q_BLOCK, topk * SLC_BLOCK))
        p_s = p_s.reshape(b, g, r, Q_BLOCK, topk, SLC_BLOCK).astype(v_sel.dtype)
        o_s = jnp.einsum('bgrqjk,bgqjkd->bqgrd', p_s, v_sel)
        k_win = lax.dynamic_slice_in_dim(kw_pad, q0, WINDOW + Q_BLOCK, 1)
        v_win = lax.dynamic_slice_in_dim(vw_pad, q0, WINDOW + Q_BLOCK, 1)
        kpos = q0 - WINDOW + jnp.arange(WINDOW + Q_BLOCK)
        dist = t[:, None] - kpos[None, :]
        w_mask = (kpos[None, :] >= 0) & (dist >= 0) & (dist < WINDOW)
        s_w = jnp.einsum('bqgrd,bkgd->bgrqk', qs, k_win) * scale
        p_w = _masked_softmax(s_w, w_mask)
        o_w = jnp.einsum('bgrqk,bkgd->bqgrd', p_w.astype(v_win.dtype), v_win)
        return gt[..., 0:1] * o_c + gt[..., 1:2] * o_s + gt[..., 2:3] * o_w

    out = lax.map(one_block, jnp.arange(s // Q_BLOCK))
    return out.transpose(1, 0, 2, 3, 4, 5).reshape(b, s, hq * d)


def _split_points():
    sizes = [GDN_WIDTH, GDN_WIDTH, GDN_WIDTH, GDN_HEADS, GDN_HEADS, GDN_WIDTH,
             NSA_WIDTH, 6 * NSA_KV_WIDTH, 3 * NSA_HEADS, 2 * D_MODEL]
    pts, acc = [], 0
    for sz in sizes[:-1]:
        acc += sz
        pts.append(acc)
    return pts


def _hybrid_mixer(h, cos, sin, w_in, conv_w, a_log, dt_bias, gdn_norm_w, gdn_w_up,
                  pos_k, k_w1, k_b1, k_w2, pos_v, v_w1, v_b1, v_w2, nsa_w_up, w_out):
    b, s, _ = h.shape
    proj = h @ w_in
    gq, gk, gv, gb, ga, gz, nq, nkv, ng, mg = jnp.split(proj, _split_points(), axis=-1)
    qkv = jax.nn.silu(_causal_conv(jnp.concatenate([gq, gk, gv], axis=-1), conv_w))
    q_a, k_a, v_a = jnp.split(qkv.astype(jnp.float32), 3, axis=-1)
    shp = (b, s, GDN_HEADS, GDN_HEAD_DIM)
    beta = jax.nn.sigmoid(gb.astype(jnp.float32))
    g_log = -jnp.exp(a_log.astype(jnp.float32)) * jax.nn.softplus(ga.astype(jnp.float32) + dt_bias.astype(jnp.float32))
    o_a = _gated_delta_rule(q_a.reshape(shp), k_a.reshape(shp), v_a.reshape(shp), g_log, beta)
    o_a = _rms_norm(o_a, gdn_norm_w) * jax.nn.silu(gz.astype(jnp.float32).reshape(shp))
    y_a = o_a.reshape(b, s, GDN_WIDTH).astype(h.dtype) @ gdn_w_up
    q_b = nq.reshape(b, s, NSA_HEADS, NSA_HEAD_DIM)
    kc, vc, ks, vs, kw, vw = [t.reshape(b, s, NSA_KV_HEADS, NSA_HEAD_DIM) for t in jnp.split(nkv, 6, axis=-1)]
    qr = _partial_rope(q_b, cos, sin)
    ks = _partial_rope(ks, cos, sin)
    kw = _partial_rope(kw, cos, sin)
    kcmp = _compress(kc, pos_k, k_w1, k_b1, k_w2)
    vcmp = _compress(vc, pos_v, v_w1, v_b1, v_w2)
    gates = jax.nn.sigmoid(ng).reshape(b, s, NSA_KV_HEADS, NSA_HEADS // NSA_KV_HEADS, 3)
    y_b = _nsa(q_b, qr, kcmp, vcmp, ks, vs, kw, vw, gates) @ nsa_w_up
    m_a, m_b = jnp.split(jax.nn.sigmoid(mg), 2, axis=-1)
    return (m_a * y_a + m_b * y_b) @ w_out


def setup_inputs(seed: int = 0) -> dict:
    key = jax.random.key(seed)
    keys = iter(jax.random.split(key, 48))

    def nrm(shape, scale):
        return jax.random.normal(next(keys), shape, jnp.float32) * scale

    def gain(shape):
        return 1.0 + nrm(shape, 0.02)

    L, D, hd = DEPTH, D_MODEL, NSA_HEAD_DIM
    dt = jnp.exp(jax.random.uniform(next(keys), (L, GDN_HEADS), jnp.float32, np.log(1e-3), np.log(1e-1)))
    return {
        "x": nrm((BATCH, SEQ, D), 1.0),
        "c": nrm((BATCH, D), 1.0),
        "ada_w": nrm((L, D, N_MOD * D), D ** -0.5),
        "ada_b": nrm((L, N_MOD * D), 0.02),
        "ffn1_norm": gain((L, D)),
        "ffn1_w_gate": nrm((L, D, D_FF), D ** -0.5),
        "ffn1_w_up": nrm((L, D, D_FF), D ** -0.5),
        "ffn1_w_down": nrm((L, D_FF, D), D_FF ** -0.5),
        "mix_norm": gain((L, D)),
        "w_in": nrm((L, D, N_IN), D ** -0.5),
        "gdn_conv_w": nrm((L, 3 * GDN_WIDTH, GDN_CONV), GDN_CONV ** -0.5),
        "gdn_a_log": jnp.log(jax.random.uniform(next(keys), (L, GDN_HEADS), jnp.float32, 1.0, 16.0)),
        "gdn_dt_bias": dt + jnp.log(-jnp.expm1(-dt)),
        "gdn_norm_w": gain((L, GDN_HEAD_DIM)),
        "gdn_w_up": nrm((L, GDN_WIDTH, D), GDN_WIDTH ** -0.5),
        "cmp_pos_k": nrm((L, CMP_BLOCK, hd), 0.02),
        "cmp_k_w1": nrm((L, CMP_BLOCK * hd, hd), (CMP_BLOCK * hd) ** -0.5),
        "cmp_k_b1": nrm((L, hd), 0.02),
        "cmp_k_w2": nrm((L, hd, hd), hd ** -0.5),
        "cmp_pos_v": nrm((L, CMP_BLOCK, hd), 0.02),
        "cmp_v_w1": nrm((L, CMP_BLOCK * hd, hd), (CMP_BLOCK * hd) ** -0.5),
        "cmp_v_b1": nrm((L, hd), 0.02),
        "cmp_v_w2": nrm((L, hd, hd), hd ** -0.5),
        "nsa_w_up": nrm((L, NSA_WIDTH, D), NSA_WIDTH ** -0.5),
        "w_out": nrm((L, D, D), D ** -0.5),
        "ffn2_norm": gain((L, D)),
        "ffn2_w_gate": nrm((L, D, D_FF), D ** -0.5),
        "ffn2_w_up": nrm((L, D, D_FF), D ** -0.5),
        "ffn2_w_down": nrm((L, D_FF, D), D_FF ** -0.5),
        "final_norm": gain((D,)),
    }


def reference(x, c, ada_w, ada_b, ffn1_norm, ffn1_w_gate, ffn1_w_up, ffn1_w_down, mix_norm, w_in,
              gdn_conv_w, gdn_a_log, gdn_dt_bias, gdn_norm_w, gdn_w_up, cmp_pos_k, cmp_k_w1, cmp_k_b1,
              cmp_k_w2, cmp_pos_v, cmp_v_w1, cmp_v_b1, cmp_v_w2, nsa_w_up, w_out, ffn2_norm,
              ffn2_w_gate, ffn2_w_up, ffn2_w_down, final_norm):
    b, s, d = x.shape
    pos = jnp.arange(s, dtype=jnp.float32)
    inv_freq = ROPE_THETA ** (-jnp.arange(0, ROPE_DIM, 2, dtype=jnp.float32) / ROPE_DIM)
    ang = pos[:, None] * inv_freq[None, :]
    cos, sin = jnp.cos(ang), jnp.sin(ang)
    c_act = jax.nn.silu(c)
    for l in range(DEPTH):
        mods = (c_act @ ada_w[l] + ada_b[l]).reshape(b, N_MOD, d)
        h = _modulate(x, ffn1_norm[l], mods[:, 0], mods[:, 1])
        x = x + 0.5 * mods[:, 2][:, None, :] * _swiglu(h, ffn1_w_gate[l], ffn1_w_up[l], ffn1_w_down[l])
        h = _modulate(x, mix_norm[l], mods[:, 3], mods[:, 4])
        y = _hybrid_mixer(h, cos, sin, w_in[l], gdn_conv_w[l], gdn_a_log[l], gdn_dt_bias[l], gdn_norm_w[l],
                          gdn_w_up[l], cmp_pos_k[l], cmp_k_w1[l], cmp_k_b1[l], cmp_k_w2[l], cmp_pos_v[l],
                          cmp_v_w1[l], cmp_v_b1[l], cmp_v_w2[l], nsa_w_up[l], w_out[l])
        x = x + mods[:, 5][:, None, :] * y
        h = _modulate(x, ffn2_norm[l], mods[:, 6], mods[:, 7])
        x = x + 0.5 * mods[:, 8][:, None, :] * _swiglu(h, ffn2_w_gate[l], ffn2_w_up[l], ffn2_w_down[l])
    return _rms_norm(x, final_norm)
```

```python
import functools

import jax
import jax.numpy as jnp
from jax import lax
from jax.experimental import pallas as pl
from jax.experimental.pallas import tpu as pltpu

F32 = jnp.float32
BF16 = jnp.bfloat16

LANES = 128
EPS = 1e-6
NEG_INF = -1e30
FORCE_BONUS = 1e3
N_MOD = 9

GDN_HEADS = 8
GDN_CONV = 4
GDN_CHUNK = 64
NSA_HEADS = 8
NSA_KV_HEADS = 2
NSA_GROUP = NSA_HEADS // NSA_KV_HEADS
CMP_BLOCK = 32
CMP_STRIDE = 16
SLC_BLOCK = 64
SLC_TOPK = 16
WINDOW = 512
ROPE_THETA = 500000.0
ROPE_DIM = LANES // 4
Q_BLOCK = 128
SEL_KEY_TILE = 512

VMEM_LIMIT = 56 * 1024 * 1024


def _dot(a, b):
    return jnp.dot(a, b, preferred_element_type=F32)


def _dot_nt(a, b):
    return lax.dot_general(a, b, (((1,), (1,)), ((), ())), preferred_element_type=F32)


def _hdot(a, b):
    return jnp.dot(a, b, precision=lax.Precision.HIGHEST, preferred_element_type=F32)


def _hdot_nt(a, b):
    return lax.dot_general(a, b, (((1,), (1,)), ((), ())), precision=lax.Precision.HIGHEST,
                           preferred_element_type=F32)


def _hdot_tn(a, b):
    return lax.dot_general(a, b, (((0,), (0,)), ((), ())), precision=lax.Precision.HIGHEST,
                           preferred_element_type=F32)


def _silu(x):
    return x * jax.nn.sigmoid(x)


def _params(*sem):
    return pltpu.CompilerParams(dimension_semantics=sem, vmem_limit_bytes=VMEM_LIMIT)


def _norm_mod(x, gain, shift, scale):
    y = x * lax.rsqrt(jnp.mean(x * x, axis=-1, keepdims=True) + EPS) * gain
    return y * (1.0 + scale) + shift


class _Layout:
    def __init__(self, d_model):
        half = d_model // LANES
        self.ma, self.mb = 0, half
        p = 2 * half
        self.gq, self.gk, self.gv, self.gz = p, p + 8, p + 16, p + 24
        self.nq = p + 32
        self.nkv = p + 40
        self.small = p + 52
        self.nblk = -(-(p + 53) // 8) * 8


def _ada_kernel(ct_ref, w_ref, b_ref, o_ref):
    act = _silu(ct_ref[...])
    w = w_ref[...]
    rows = [jnp.sum(w * act[:, b:b + 1], axis=0, keepdims=True) for b in range(act.shape[1])]
    o_ref[...] = jnp.concatenate(rows, axis=0) + b_ref[...]


def _ada(c, w, bias):
    bsz, d = c.shape
    n = w.shape[1]
    tn = min(1024, d)
    assert n % tn == 0
    return pl.pallas_call(
        _ada_kernel,
        grid=(n // tn,),
        in_specs=[pl.BlockSpec((d, bsz), lambda j: (0, 0)),
                  pl.BlockSpec((d, tn), lambda j: (0, j)),
                  pl.BlockSpec((1, tn), lambda j: (0, j))],
        out_specs=pl.BlockSpec((bsz, tn), lambda j: (0, j)),
        out_shape=jax.ShapeDtypeStruct((bsz, n), F32),
        compiler_params=_params("arbitrary"),
        name="ada",
    )(c.T, w, bias.reshape(1, n))


def _ffn_kernel(x_ref, mod_ref, gain_ref, wg_ref, wu_ref, wd_ref, fin_ref, o_ref, h_scr, acc_scr,
                *, mod_base, final):
    j = pl.program_id(1)

    @pl.when(j == 0)
    def _():
        m = mod_ref[0]
        h = _norm_mod(x_ref[...], gain_ref[...], m[mod_base:mod_base + 1], m[mod_base + 1:mod_base + 2])
        h_scr[...] = h.astype(BF16)
        acc_scr[...] = jnp.zeros_like(acc_scr)

    h = h_scr[...]
    a = _silu(_dot(h, wg_ref[...])) * _dot(h, wu_ref[...])
    acc_scr[...] += _dot(a.astype(BF16), wd_ref[...])

    @pl.when(j == pl.num_programs(1) - 1)
    def _():
        m = mod_ref[0]
        y = x_ref[...] + 0.5 * m[mod_base + 2:mod_base + 3] * acc_scr[...]
        if final:
            y = y * lax.rsqrt(jnp.mean(y * y, axis=-1, keepdims=True) + EPS) * fin_ref[...]
        o_ref[...] = y


def _ffn(x, mods, gain, wg, wu, wd, fin, *, mod_base, final, seq, tm=512, tf=512):
    t, d = x.shape
    dff = wg.shape[1]
    tm, tf = min(tm, seq), min(tf, dff)
    per_b = seq // tm
    return pl.pallas_call(
        functools.partial(_ffn_kernel, mod_base=mod_base, final=final),
        grid=(t // tm, dff // tf),
        in_specs=[pl.BlockSpec((tm, d), lambda i, j: (i, 0)),
                  pl.BlockSpec((1, N_MOD, d), lambda i, j: (i // per_b, 0, 0)),
                  pl.BlockSpec((1, d), lambda i, j: (0, 0)),
                  pl.BlockSpec((d, tf), lambda i, j: (0, j)),
                  pl.BlockSpec((d, tf), lambda i, j: (0, j)),
                  pl.BlockSpec((tf, d), lambda i, j: (j, 0)),
                  pl.BlockSpec((1, d), lambda i, j: (0, 0))],
        out_specs=pl.BlockSpec((tm, d), lambda i, j: (i, 0)),
        out_shape=jax.ShapeDtypeStruct((t, d), F32),
        scratch_shapes=[pltpu.VMEM((tm, d), BF16), pltpu.VMEM((tm, d), F32)],
        compiler_params=_params("parallel", "arbitrary"),
        name="ffn_final" if final else "ffn",
    )(x, mods, gain.reshape(1, d), wg, wu, wd, fin.reshape(1, d))


def _inproj_kernel(x_ref, mod_ref, gain_ref, w_ref, o_ref, h_scr):
    @pl.when(pl.program_id(1) == 0)
    def _():
        m = mod_ref[0]
        h_scr[...] = _norm_mod(x_ref[...], gain_ref[...], m[3:4], m[4:5]).astype(BF16)

    res = _dot(h_scr[...], w_ref[...])
    for c in range(o_ref.shape[0]):
        o_ref[c] = res[:, c * LANES:(c + 1) * LANES]


def _inproj(x, mods, gain, w, *, seq, tm=512, nb=8):
    t, d = x.shape
    nblk = w.shape[1] // LANES
    tm = min(tm, seq)
    per_b = seq // tm
    return pl.pallas_call(
        _inproj_kernel,
        grid=(t // tm, nblk // nb),
        in_specs=[pl.BlockSpec((tm, d), lambda i, j: (i, 0)),
                  pl.BlockSpec((1, N_MOD, d), lambda i, j: (i // per_b, 0, 0)),
                  pl.BlockSpec((1, d), lambda i, j: (0, 0)),
                  pl.BlockSpec((d, nb * LANES), lambda i, j: (0, j))],
        out_specs=pl.BlockSpec((nb, tm, LANES), lambda i, j: (j, i, 0)),
        out_shape=jax.ShapeDtypeStruct((nblk, t, LANES), F32),
        scratch_shapes=[pltpu.VMEM((tm, d), BF16)],
        compiler_params=_params("parallel", "arbitrary"),
        name="inproj",
    )(x, mods, gain.reshape(1, d), w)


def _gdn_kernel(q_ref, k_ref, v_ref, gz_ref, sm_ref, cwq_ref, cwk_ref, cwv_ref, alog_ref, dtb_ref, nw_ref,
                o_ref, s_scr):
    head = pl.program_id(1)
    c = GDN_CHUNK
    seq = q_ref.shape[1]
    lane = lax.broadcasted_iota(jnp.int32, (c, LANES), 1)
    row = lax.broadcasted_iota(jnp.int32, (c, c), 0)
    col = lax.broadcasted_iota(jnp.int32, (c, c), 1)
    causal = row >= col
    strict = row > col
    tril = causal.astype(F32)
    triu = (row <= col).astype(F32)
    eye = (row == col).astype(F32)
    ones = jnp.ones((c, c), F32)
    neg_decay_rate = -jnp.exp(alog_ref[...])
    dt_bias = dtb_ref[...]
    norm_w = nw_ref[...]
    s_scr[...] = jnp.zeros_like(s_scr)

    def conv_silu(ref, w_ref, r0, n):
        cur = ref[0, pl.ds(r0, c), :]
        prev = ref[0, pl.ds(jnp.maximum(r0 - 8, 0), 8), :]
        prev = jnp.where(n > 0, prev, 0.0)
        xc = jnp.concatenate([prev, cur], axis=0)
        w = w_ref[...]
        y = w[0:1] * xc[5:5 + c] + w[1:2] * xc[6:6 + c] + w[2:3] * xc[7:7 + c] + w[3:4] * xc[8:8 + c]
        return _silu(y)

    def body(n, carry):
        r0 = pl.multiple_of(n * c, c)
        q = conv_silu(q_ref, cwq_ref, r0, n)
        k = conv_silu(k_ref, cwk_ref, r0, n)
        v = conv_silu(v_ref, cwv_ref, r0, n)
        q = q * lax.rsqrt(jnp.sum(q * q, axis=-1, keepdims=True) + EPS) * (LANES ** -0.5)
        k = k * lax.rsqrt(jnp.sum(k * k, axis=-1, keepdims=True) + EPS)

        sm = sm_ref[0, pl.ds(r0, c), :]
        xg = sm + dt_bias
        softplus = jnp.maximum(xg, 0.0) + jnp.log1p(jnp.exp(-jnp.abs(xg)))
        beta = jnp.sum(jnp.where(lane == head, jax.nn.sigmoid(sm), 0.0), axis=1, keepdims=True)
        g = jnp.sum(jnp.where(lane == head + GDN_HEADS, neg_decay_rate * softplus, 0.0), axis=1, keepdims=True)
        g_b = jnp.broadcast_to(g, (c, LANES))
        gc = _hdot(tril, g_b)
        gc_row = _hdot(ones, g_b[:, :c] * triu)
        decay = jnp.exp(jnp.where(causal, gc[:, :c] - gc_row, NEG_INF))

        kb = k * beta
        lmat = jnp.where(strict, _hdot_nt(kb, k) * decay, 0.0)
        p = -lmat
        tmat = eye + p
        for _ in range(5):
            p = _hdot(p, p)
            tmat = tmat + _hdot(tmat, p)
        e_gc = jnp.exp(gc)
        u = _hdot(tmat, v * beta)
        w = _hdot(tmat, kb * e_gc)
        a_intra = jnp.where(causal, _hdot_nt(q, k) * decay, 0.0)
        gc_last = gc[c - 1:c, :]
        k_d = k * jnp.exp(gc_last - gc)

        state = s_scr[...]
        v_new = u - _hdot(w, state)
        o = _hdot(q * e_gc, state) + _hdot(a_intra, v_new)
        s_scr[...] = state * jnp.exp(gc_last) + _hdot_tn(k_d, v_new)

        o = o * lax.rsqrt(jnp.mean(o * o, axis=-1, keepdims=True) + EPS) * norm_w
        o_ref[pl.ds(r0, c), :] = o * _silu(gz_ref[0, pl.ds(r0, c), :])
        return carry

    lax.fori_loop(0, seq // c, body, 0)


def _gdn(proj, lay, conv_wt, alog_l, dtb_l, norm_w, *, bsz, seq):
    def head_spec(base):
        return pl.BlockSpec((1, seq, LANES), lambda b, h: (base + h, b, 0))

    def conv_spec(base):
        return pl.BlockSpec((GDN_CONV, LANES), lambda b, h: (0, base + h))

    vec = pl.BlockSpec((1, LANES), lambda b, h: (0, 0))
    return pl.pallas_call(
        _gdn_kernel,
        grid=(bsz, GDN_HEADS),
        in_specs=[head_spec(lay.gq), head_spec(lay.gk), head_spec(lay.gv), head_spec(lay.gz),
                  pl.BlockSpec((1, seq, LANES), lambda b, h: (lay.small, b, 0)),
                  conv_spec(0), conv_spec(GDN_HEADS), conv_spec(2 * GDN_HEADS), vec, vec, vec],
        out_specs=pl.BlockSpec((seq, LANES), lambda b, h: (b, h)),
        out_shape=jax.ShapeDtypeStruct((bsz * seq, GDN_HEADS * LANES), F32),
        scratch_shapes=[pltpu.VMEM((LANES, LANES), F32)],
        compiler_params=_params("parallel", "arbitrary"),
        name="gdn",
    )(proj, proj, proj, proj, proj, conv_wt, conv_wt, conv_wt, alog_l, dtb_l, norm_w)


def _compress_kernel(x_ref, pos_ref, w1_ref, b1_ref, w2_ref, o_ref):
    x = x_ref[0, 0]
    half = x.shape[1]
    pos = pos_ref[0]
    first = _hdot(x + pos[0:1], w1_ref[0, 0:half, :])
    second = _hdot(x + pos[1:2], w1_ref[0, half:2 * half, :])
    rows = x.shape[0]
    hid = first + pltpu.roll(second, rows - 1, 0) + b1_ref[0]
    o_ref[0, 0] = _hdot(_silu(hid), w2_ref[0])


def _compress(proj, lay, pos, w1, b1, w2, *, bsz, seq):
    nblk = proj.shape[0]
    rows = seq // CMP_STRIDE
    wide = CMP_STRIDE * LANES
    x = proj.reshape(nblk, bsz, rows, wide)
    nkv = 2 * NSA_KV_HEADS
    return pl.pallas_call(
        _compress_kernel,
        grid=(bsz, nkv),
        in_specs=[pl.BlockSpec((1, 1, rows, wide), lambda b, j: (lay.nkv + j, b, 0, 0)),
                  pl.BlockSpec((1, 2, wide), lambda b, j: (j // NSA_KV_HEADS, 0, 0)),
                  pl.BlockSpec((1, 2 * wide, LANES), lambda b, j: (j // NSA_KV_HEADS, 0, 0)),
                  pl.BlockSpec((1, 1, LANES), lambda b, j: (j // NSA_KV_HEADS, 0, 0)),
                  pl.BlockSpec((1, LANES, LANES), lambda b, j: (j // NSA_KV_HEADS, 0, 0))],
        out_specs=pl.BlockSpec((1, 1, rows, LANES), lambda b, j: (j, b, 0, 0)),
        out_shape=jax.ShapeDtypeStruct((nkv, bsz, rows, LANES), F32),
        compiler_params=_params("parallel", "arbitrary"),
        name="compress",
    )(x, pos, w1, b1, w2)


def _rope_tables(seq):
    pos = jnp.arange(seq, dtype=F32)
    inv_freq = ROPE_THETA ** (-jnp.arange(0, ROPE_DIM, 2, dtype=F32) / ROPE_DIM)
    ang = pos[:, None] * inv_freq[None, :]
    cos, sin = jnp.cos(ang), jnp.sin(ang)
    half = ROPE_DIM // 2
    rest = LANES - ROPE_DIM
    mult = jnp.concatenate([cos, cos, jnp.ones((seq, rest), F32)], axis=1)
    from_upper = jnp.concatenate([-sin, jnp.zeros((seq, LANES - half), F32)], axis=1)
    from_lower = jnp.concatenate([jnp.zeros((seq, half), F32), sin, jnp.zeros((seq, rest), F32)], axis=1)
    return jnp.stack([mult, from_upper, from_lower])


def _rope(x, tab):
    half = ROPE_DIM // 2
    return (x * tab[0] + pltpu.roll(x, LANES - half, x.ndim - 1) * tab[1]
            + pltpu.roll(x, half, x.ndim - 1) * tab[2])


def _kvprep_kernel(x_ref, tab_ref, o_ref):
    x = x_ref[0]
    is_key = (pl.program_id(0) % 4) < NSA_KV_HEADS
    o_ref[0] = jnp.where(is_key, _rope(x, tab_ref[...]), x).astype(BF16)


def _kvprep(proj, lay, tabs, *, bsz, seq, tr=1024):
    t = bsz * seq
    tr = min(tr, seq)
    per_b = seq // tr
    n = 4 * NSA_KV_HEADS
    return pl.pallas_call(
        _kvprep_kernel,
        grid=(n, t // tr),
        in_specs=[pl.BlockSpec((1, tr, LANES), lambda j, i: (lay.nkv + 2 * NSA_KV_HEADS + j, i, 0)),
                  pl.BlockSpec((3, tr, LANES), lambda j, i: (0, i % per_b, 0))],
        out_specs=pl.BlockSpec((1, tr, LANES), lambda j, i: (j, i, 0)),
        out_shape=jax.ShapeDtypeStruct((n, t, LANES), BF16),
        compiler_params=_params("parallel", "arbitrary"),
        name="kvprep",
    )(proj, tabs)


def _softmax_rows(s, mask):
    s = jnp.where(mask, s, NEG_INF)
    m = jnp.max(s, axis=-1, keepdims=True)
    e = jnp.where(mask, jnp.exp(s - m), 0.0)
    den = jnp.sum(e, axis=-1, keepdims=True)
    return e / jnp.where(den > 0, den, 1.0)


def _nsa_kernel(q_ref, tab_ref, kc_ref, vc_ref, ks_ref, vs_ref, kw_ref, vw_ref, sm_ref, o_ref,
                m_scr, l_scr, acc_scr):
    grp = pl.program_id(1)
    qb = pl.program_id(2)
    nq = Q_BLOCK
    rows = NSA_GROUP * nq
    seq = ks_ref.shape[1]
    n_slc = seq // SLC_BLOCK
    scale = LANES ** -0.5
    q0 = qb * nq

    q_raw = q_ref[...]
    q_rot = _rope(q_raw, tab_ref[...]).reshape(rows, LANES).astype(BF16)
    q_raw = q_raw.reshape(rows, LANES)
    t_row = q0 + lax.broadcasted_iota(jnp.int32, (rows, 1), 0) % nq
    t_tok = q0 + lax.broadcasted_iota(jnp.int32, (nq, 1), 0)

    n_cmp = kc_ref.shape[2]
    cmp_end = lax.broadcasted_iota(jnp.int32, (1, n_cmp), 1) * CMP_STRIDE + (CMP_BLOCK - 1)
    p_c = _softmax_rows(_hdot_nt(q_raw, kc_ref[0, 0]) * scale, cmp_end <= t_row)
    o_c = _hdot(p_c, vc_ref[0, 0])

    imp_cmp = p_c[0:nq]
    for r in range(1, NSA_GROUP):
        imp_cmp = imp_cmp + p_c[r * nq:(r + 1) * nq]
    per_slc = SLC_BLOCK // CMP_STRIDE
    pool = (lax.broadcasted_iota(jnp.int32, (n_cmp, n_slc), 0) // per_slc
            == lax.broadcasted_iota(jnp.int32, (n_cmp, n_slc), 1)).astype(F32)
    imp = _hdot(imp_cmp, pool)
    blk = lax.broadcasted_iota(jnp.int32, (1, n_slc), 1)
    t_blk = t_tok // SLC_BLOCK
    visible = blk * SLC_BLOCK <= t_tok
    forced = (blk == 0) | (blk == t_blk) | (blk == t_blk - 1)
    score = jnp.where(visible, imp + jnp.where(forced, FORCE_BONUS, 0.0), NEG_INF)
    rank = jnp.zeros((nq, n_slc), jnp.int32)
    for i in range(n_slc):
        s_i = score[:, i:i + 1]
        ahead = (s_i > score) | ((s_i == score) & (blk > i))
        rank = rank + ahead.astype(jnp.int32)
    sel = ((rank < min(SLC_TOPK, n_slc)) & visible).astype(BF16)
    sel_rows = jnp.concatenate([sel] * NSA_GROUP, axis=0)

    tk = min(SEL_KEY_TILE, seq)
    m_scr[...] = jnp.full_like(m_scr, NEG_INF)
    l_scr[...] = jnp.zeros_like(l_scr)
    acc_scr[...] = jnp.zeros_like(acc_scr)

    def sel_step(kt, carry):
        k0 = pl.multiple_of(kt * tk, tk)
        k = ks_ref[0, pl.ds(k0, tk), :]
        v = vs_ref[0, pl.ds(k0, tk), :]
        s = _dot_nt(q_rot, k) * scale
        kpos = k0 + lax.broadcasted_iota(jnp.int32, (1, tk), 1)
        expand = ((k0 + lax.broadcasted_iota(jnp.int32, (n_slc, tk), 1)) // SLC_BLOCK
                  == lax.broadcasted_iota(jnp.int32, (n_slc, tk), 0)).astype(BF16)
        mask = (_dot(sel_rows, expand) > 0.5) & (kpos <= t_row)
        s = jnp.where(mask, s, NEG_INF)
        m_old = m_scr[...]
        m_new = jnp.maximum(m_old, jnp.max(s, axis=-1, keepdims=True))
        alpha = jnp.exp(m_old - m_new)
        p = jnp.where(mask, jnp.exp(s - m_new), 0.0)
        l_scr[...] = alpha * l_scr[...] + jnp.sum(p, axis=-1, keepdims=True)
        acc_scr[...] = alpha * acc_scr[...] + _dot(p.astype(BF16), v)
        m_scr[...] = m_new
        return carry

    lax.fori_loop(0, (q0 + nq + tk - 1) // tk, sel_step, 0)
    den = l_scr[...]
    o_s = acc_scr[...] / jnp.where(den > 0, den, 1.0)

    span = min(WINDOW + nq, seq)
    w0 = pl.multiple_of(jnp.maximum(q0 + nq - span, 0), nq)
    kpos = w0 + lax.broadcasted_iota(jnp.int32, (1, span), 1)
    dist = t_row - kpos
    p_w = _softmax_rows(_dot_nt(q_rot, kw_ref[0, pl.ds(w0, span), :]) * scale, (dist >= 0) & (dist < WINDOW))
    o_w = _dot(p_w.astype(BF16), vw_ref[0, pl.ds(w0, span), :])

    gates = jax.nn.sigmoid(sm_ref[0])
    lane = lax.broadcasted_iota(jnp.int32, (nq, LANES), 1)
    base = 2 * GDN_HEADS + grp * (3 * NSA_GROUP)
    for r in range(NSA_GROUP):
        out = jnp.zeros((nq, LANES), F32)
        for j, branch in enumerate((o_c, o_s, o_w)):
            gate = jnp.sum(jnp.where(lane == base + 3 * r + j, gates, 0.0), axis=1, keepdims=True)
            out = out + gate * branch[r * nq:(r + 1) * nq]
        o_ref[:, r * LANES:(r + 1) * LANES] = out


def _nsa(proj, lay, tabs, kvcmp, kvatt, *, bsz, seq):
    nq = Q_BLOCK
    nqb = seq // nq
    rows = NSA_GROUP * nq
    n_cmp = kvcmp.shape[2]

    def cmp_spec(base):
        return pl.BlockSpec((1, 1, n_cmp, LANES), lambda b, g, i: (base + g, b, 0, 0))

    def kv_spec(base):
        return pl.BlockSpec((1, seq, LANES), lambda b, g, i: (base + g, b, 0))

    return pl.pallas_call(
        _nsa_kernel,
        grid=(bsz, NSA_KV_HEADS, nqb),
        in_specs=[pl.BlockSpec((NSA_GROUP, nq, LANES), lambda b, g, i: (lay.nq // NSA_GROUP + g, b * nqb + i, 0)),
                  pl.BlockSpec((3, nq, LANES), lambda b, g, i: (0, i, 0)),
                  cmp_spec(0), cmp_spec(NSA_KV_HEADS),
                  kv_spec(0), kv_spec(NSA_KV_HEADS), kv_spec(2 * NSA_KV_HEADS), kv_spec(3 * NSA_KV_HEADS),
                  pl.BlockSpec((1, nq, LANES), lambda b, g, i: (lay.small, b * nqb + i, 0))],
        out_specs=pl.BlockSpec((nq, NSA_GROUP * LANES), lambda b, g, i: (b * nqb + i, g)),
        out_shape=jax.ShapeDtypeStruct((bsz * seq, NSA_HEADS * LANES), F32),
        scratch_shapes=[pltpu.VMEM((rows, 1), F32), pltpu.VMEM((rows, 1), F32), pltpu.VMEM((rows, LANES), F32)],
        compiler_params=_params("parallel", "parallel", "arbitrary"),
        name="nsa",
    )(proj, tabs, kvcmp, kvcmp, kvatt, kvatt, kvatt, kvatt, proj)


def _outproj_kernel(oa_ref, ob_ref, ma_ref, mb_ref, x_ref, mod_ref, wa_ref, wb_ref, wo_ref, o_ref, mrg_scr):
    ya = _dot(oa_ref[...].astype(BF16), wa_ref[...])
    yb = _dot(ob_ref[...].astype(BF16), wb_ref[...])
    for c in range(ma_ref.shape[0]):
        sl = slice(c * LANES, (c + 1) * LANES)
        merged = jax.nn.sigmoid(ma_ref[c]) * ya[:, sl] + jax.nn.sigmoid(mb_ref[c]) * yb[:, sl]
        mrg_scr[:, sl] = merged.astype(BF16)
    o_ref[...] = x_ref[...] + mod_ref[0][5:6] * _dot(mrg_scr[...], wo_ref[...])


def _outproj(o_a, o_b, proj, lay, x, mods, wa, wb, wo, *, seq, tm=256):
    t, d = x.shape
    half = d // LANES
    tm = min(tm, seq)
    per_b = seq // tm

    def resident(shape):
        return pl.BlockSpec(shape, lambda i: (0, 0), pipeline_mode=pl.Buffered(1))

    return pl.pallas_call(
        _outproj_kernel,
        grid=(t // tm,),
        in_specs=[pl.BlockSpec((tm, o_a.shape[1]), lambda i: (i, 0)),
                  pl.BlockSpec((tm, o_b.shape[1]), lambda i: (i, 0)),
                  pl.BlockSpec((half, tm, LANES), lambda i: (lay.ma // half, i, 0)),
                  pl.BlockSpec((half, tm, LANES), lambda i: (lay.mb // half, i, 0)),
                  pl.BlockSpec((tm, d), lambda i: (i, 0)),
                  pl.BlockSpec((1, N_MOD, d), lambda i: (i // per_b, 0, 0)),
                  resident(wa.shape), resident(wb.shape), resident(wo.shape)],
        out_specs=pl.BlockSpec((tm, d), lambda i: (i, 0)),
        out_shape=jax.ShapeDtypeStruct((t, d), F32),
        scratch_shapes=[pltpu.VMEM((tm, d), BF16)],
        compiler_params=_params("parallel"),
        name="outproj",
    )(o_a, o_b, proj, proj, x, mods, wa, wb, wo)


def _rearranged_w_in(w_in, lay, d):
    gw = GDN_HEADS * LANES
    nw = NSA_HEADS * LANES
    kvw = 6 * NSA_KV_HEADS * LANES
    o_gb = 3 * gw
    o_gz = o_gb + 2 * GDN_HEADS
    o_nq = o_gz + gw
    o_nkv = o_nq + nw
    o_ng = o_nkv + kvw
    o_mg = o_ng + 3 * NSA_HEADS
    small = jnp.concatenate([w_in[:, o_gb:o_gz], w_in[:, o_ng:o_mg]], axis=1)
    pad = lay.nblk * LANES - (2 * d + 3 * gw + gw + nw + kvw + small.shape[1])
    cols = [w_in[:, o_mg:o_mg + 2 * d], w_in[:, 0:o_gb], w_in[:, o_gz:o_nq], w_in[:, o_nq:o_nkv],
            w_in[:, o_nkv:o_ng], small, jnp.zeros((d, pad), w_in.dtype)]
    return jnp.concatenate(cols, axis=1).astype(BF16)


def _lane_vec(v, offset):
    return jnp.zeros((1, LANES), F32).at[0, offset:offset + v.shape[0]].set(v.astype(F32))


def kernel(x, c, ada_w, ada_b, ffn1_norm, ffn1_w_gate, ffn1_w_up, ffn1_w_down, mix_norm, w_in, gdn_conv_w, gdn_a_log, gdn_dt_bias, gdn_norm_w, gdn_w_up, cmp_pos_k, cmp_k_w1, cmp_k_b1, cmp_k_w2, cmp_pos_v, cmp_v_w1, cmp_v_b1, cmp_v_w2, nsa_w_up, w_out, ffn2_norm, ffn2_w_gate, ffn2_w_up, ffn2_w_down, final_norm):
    bsz, seq, d = x.shape
    depth = ada_w.shape[0]
    lay = _Layout(d)
    tabs = _rope_tables(seq)
    xf = x.reshape(bsz * seq, d)
    for l in range(depth):
        last = l == depth - 1
        mods = _ada(c, ada_w[l], ada_b[l]).reshape(bsz, N_MOD, d)
        xf = _ffn(xf, mods, ffn1_norm[l], ffn1_w_gate[l].astype(BF16), ffn1_w_up[l].astype(BF16),
                  ffn1_w_down[l].astype(BF16), final_norm, mod_base=0, final=False, seq=seq)
        proj = _inproj(xf, mods, mix_norm[l], _rearranged_w_in(w_in[l], lay, d), seq=seq)
        o_a = _gdn(proj, lay, gdn_conv_w[l].T, _lane_vec(gdn_a_log[l], GDN_HEADS),
                   _lane_vec(gdn_dt_bias[l], GDN_HEADS), gdn_norm_w[l].reshape(1, LANES), bsz=bsz, seq=seq)
        wide = CMP_STRIDE * LANES
        kvcmp = _compress(proj, lay,
                          jnp.stack([cmp_pos_k[l], cmp_pos_v[l]]).reshape(2, 2, wide),
                          jnp.stack([cmp_k_w1[l], cmp_v_w1[l]]),
                          jnp.stack([cmp_k_b1[l], cmp_v_b1[l]]).reshape(2, 1, LANES),
                          jnp.stack([cmp_k_w2[l], cmp_v_w2[l]]), bsz=bsz, seq=seq)
        kvatt = _kvprep(proj, lay, tabs, bsz=bsz, seq=seq)
        o_b = _nsa(proj, lay, tabs, kvcmp, kvatt, bsz=bsz, seq=seq)
        xf = _outproj(o_a, o_b, proj, lay, xf, mods, gdn_w_up[l].astype(BF16), nsa_w_up[l].astype(BF16),
                      w_out[l].astype(BF16), seq=seq)
        xf = _ffn(xf, mods, ffn2_norm[l], ffn2_w_gate[l].astype(BF16), ffn2_w_up[l].astype(BF16),
                  ffn2_w_down[l].astype(BF16), final_norm, mod_base=6, final=last, seq=seq)
    return xf.reshape(bsz, seq, d)
```

```python
import functools

import jax
import jax.numpy as jnp
from jax import lax
from jax.experimental import pallas as pl
from jax.experimental.pallas import tpu as pltpu

F32 = jnp.float32
BF16 = jnp.bfloat16

LANES = 128
EPS = 1e-6
NEG_INF = -1e30
FORCE_BONUS = 1e3
N_MOD = 9

GDN_HEADS = 8
GDN_CONV = 4
GDN_CHUNK = 128
GDN_INV_BASE = 16
NSA_HEADS = 8
NSA_KV_HEADS = 2
NSA_GROUP = NSA_HEADS // NSA_KV_HEADS
CMP_BLOCK = 32
CMP_STRIDE = 16
SLC_BLOCK = 64
SLC_TOPK = 16
WINDOW = 512
ROPE_THETA = 500000.0
ROPE_DIM = LANES // 4
Q_BLOCK = 128
SEL_KEY_TILE = 512

VMEM_LIMIT = 56 * 1024 * 1024


def _dot(a, b):
    return jnp.dot(a, b, preferred_element_type=F32)


def _dot_nt(a, b):
    return lax.dot_general(a, b, (((1,), (1,)), ((), ())), preferred_element_type=F32)


def _hdot(a, b):
    return jnp.dot(a, b, precision=lax.Precision.HIGHEST, preferred_element_type=F32)


def _hdot_nt(a, b):
    return lax.dot_general(a, b, (((1,), (1,)), ((), ())), precision=lax.Precision.HIGHEST,
                           preferred_element_type=F32)


def _hdot_tn(a, b):
    return lax.dot_general(a, b, (((0,), (0,)), ((), ())), precision=lax.Precision.HIGHEST,
                           preferred_element_type=F32)


def _silu(x):
    return x * jax.nn.sigmoid(x)


def _params(*sem):
    return pltpu.CompilerParams(dimension_semantics=sem, vmem_limit_bytes=VMEM_LIMIT)


def _norm_mod(x, gain, shift, scale):
    y = x * lax.rsqrt(jnp.mean(x * x, axis=-1, keepdims=True) + EPS) * gain
    return y * (1.0 + scale) + shift


class _Layout:
    def __init__(self, d_model):
        half = d_model // LANES
        self.ma, self.mb = 0, half
        p = 2 * half
        self.gq, self.gk, self.gv, self.gz = p, p + 8, p + 16, p + 24
        self.nq = p + 32
        self.nkv = p + 40
        self.small = p + 52
        self.nblk = -(-(p + 53) // 8) * 8


def _ada_kernel(ct_ref, w_ref, b_ref, o_ref):
    act = _silu(ct_ref[...])
    w = w_ref[...]
    rows = [jnp.sum(w * act[:, b:b + 1], axis=0, keepdims=True) for b in range(act.shape[1])]
    o_ref[...] = jnp.concatenate(rows, axis=0) + b_ref[...]


def _ada(c, w, bias):
    bsz, d = c.shape
    n = w.shape[1]
    tn = min(1024, d)
    assert n % tn == 0
    return pl.pallas_call(
        _ada_kernel,
        grid=(n // tn,),
        in_specs=[pl.BlockSpec((d, bsz), lambda j: (0, 0)),
                  pl.BlockSpec((d, tn), lambda j: (0, j)),
                  pl.BlockSpec((1, tn), lambda j: (0, j))],
        out_specs=pl.BlockSpec((bsz, tn), lambda j: (0, j)),
        out_shape=jax.ShapeDtypeStruct((bsz, n), F32),
        compiler_params=_params("arbitrary"),
        name="ada",
    )(c.T, w, bias.reshape(1, n))


def _ffn_kernel(x_ref, mod_ref, gain_ref, wg_ref, wu_ref, wd_ref, fin_ref, o_ref, h_scr, acc_scr,
                *, mod_base, final):
    j = pl.program_id(1)

    @pl.when(j == 0)
    def _():
        m = mod_ref[0]
        h = _norm_mod(x_ref[...], gain_ref[...], m[mod_base:mod_base + 1], m[mod_base + 1:mod_base + 2])
        h_scr[...] = h.astype(BF16)
        acc_scr[...] = jnp.zeros_like(acc_scr)

    h = h_scr[...]
    a = _silu(_dot(h, wg_ref[...])) * _dot(h, wu_ref[...])
    acc_scr[...] += _dot(a.astype(BF16), wd_ref[...])

    @pl.when(j == pl.num_programs(1) - 1)
    def _():
        m = mod_ref[0]
        y = x_ref[...] + 0.5 * m[mod_base + 2:mod_base + 3] * acc_scr[...]
        if final:
            y = y * lax.rsqrt(jnp.mean(y * y, axis=-1, keepdims=True) + EPS) * fin_ref[...]
        o_ref[...] = y


def _ffn(x, mods, gain, wg, wu, wd, fin, *, mod_base, final, seq, tm=512, tf=512):
    t, d = x.shape
    dff = wg.shape[1]
    tm, tf = min(tm, seq), min(tf, dff)
    per_b = seq // tm
    return pl.pallas_call(
        functools.partial(_ffn_kernel, mod_base=mod_base, final=final),
        grid=(t // tm, dff // tf),
        in_specs=[pl.BlockSpec((tm, d), lambda i, j: (i, 0)),
                  pl.BlockSpec((1, N_MOD, d), lambda i, j: (i // per_b, 0, 0)),
                  pl.BlockSpec((1, d), lambda i, j: (0, 0)),
                  pl.BlockSpec((d, tf), lambda i, j: (0, j)),
                  pl.BlockSpec((d, tf), lambda i, j: (0, j)),
                  pl.BlockSpec((tf, d), lambda i, j: (j, 0)),
                  pl.BlockSpec((1, d), lambda i, j: (0, 0))],
        out_specs=pl.BlockSpec((tm, d), lambda i, j: (i, 0)),
        out_shape=jax.ShapeDtypeStruct((t, d), F32),
        scratch_shapes=[pltpu.VMEM((tm, d), BF16), pltpu.VMEM((tm, d), F32)],
        compiler_params=_params("parallel", "arbitrary"),
        name="ffn_final" if final else "ffn",
    )(x, mods, gain.reshape(1, d), wg, wu, wd, fin.reshape(1, d))


def _inproj_kernel(x_ref, mod_ref, gain_ref, w_ref, o_ref, h_scr):
    @pl.when(pl.program_id(1) == 0)
    def _():
        m = mod_ref[0]
        h_scr[...] = _norm_mod(x_ref[...], gain_ref[...], m[3:4], m[4:5]).astype(BF16)

    res = _dot(h_scr[...], w_ref[...])
    for c in range(o_ref.shape[0]):
        o_ref[c] = res[:, c * LANES:(c + 1) * LANES]


def _inproj(x, mods, gain, w, *, seq, tm=512, nb=8):
    t, d = x.shape
    nblk = w.shape[1] // LANES
    tm = min(tm, seq)
    per_b = seq // tm
    return pl.pallas_call(
        _inproj_kernel,
        grid=(t // tm, nblk // nb),
        in_specs=[pl.BlockSpec((tm, d), lambda i, j: (i, 0)),
                  pl.BlockSpec((1, N_MOD, d), lambda i, j: (i // per_b, 0, 0)),
                  pl.BlockSpec((1, d), lambda i, j: (0, 0)),
                  pl.BlockSpec((d, nb * LANES), lambda i, j: (0, j))],
        out_specs=pl.BlockSpec((nb, tm, LANES), lambda i, j: (j, i, 0)),
        out_shape=jax.ShapeDtypeStruct((nblk, t, LANES), F32),
        scratch_shapes=[pltpu.VMEM((tm, d), BF16)],
        compiler_params=_params("parallel", "arbitrary"),
        name="inproj",
    )(x, mods, gain.reshape(1, d), w)


def _gdn_kernel(q_ref, k_ref, v_ref, gz_ref, sm_ref, cw_ref, alog_ref, dtb_ref, nw_ref, o_ref, s_scr, tail_scr):
    c = GDN_CHUNK
    rows = q_ref.shape[1]

    @pl.when(pl.program_id(1) == 0)
    def _():
        s_scr[...] = jnp.zeros_like(s_scr)
        tail_scr[...] = jnp.zeros_like(tail_scr)

    row = lax.broadcasted_iota(jnp.int32, (c, c), 0)
    col = lax.broadcasted_iota(jnp.int32, (c, c), 1)
    causal = row >= col
    strict = row > col
    tril = causal.astype(F32)
    eye = (row == col).astype(F32)
    base = GDN_INV_BASE
    same_base = (row // base) == (col // base)
    merges = []
    width = base
    while width < c:
        merges.append(((row // (2 * width)) == (col // (2 * width))) & ((row // width) != (col // width)))
        width *= 2
    neg_decay_rate = -jnp.exp(alog_ref[...])
    dt_bias = dtb_ref[...]
    norm_w = nw_ref[...]
    conv_w = cw_ref[...]

    def mm(a, b):
        return _dot(a.astype(BF16), b.astype(BF16))

    heads = range(GDN_HEADS)

    def unit_lower_inverse(lmat):
        p = [-jnp.where(same_base, lmat[h], 0.0) for h in heads]
        t = [eye + p[h] for h in heads]
        span = 2
        while span < base:
            p = [mm(p[h], p[h]) for h in heads]
            t = [t[h] + mm(t[h], p[h]) for h in heads]
            span *= 2
        for pair in merges:
            x = [mm(t[h], jnp.where(pair, lmat[h], 0.0)) for h in heads]
            t = [t[h] - mm(x[h], t[h]) for h in heads]
        return t

    def conv_silu(ref, which, h, r0, n):
        cur = ref[h, pl.ds(r0, c), :]
        prev = jnp.where(n > 0, ref[h, pl.ds(jnp.maximum(r0 - 8, 0), 8), :], tail_scr[which, h])
        xc = jnp.concatenate([prev, cur], axis=0)
        ch = (which * GDN_HEADS + h) * LANES
        w = conv_w[:, ch:ch + LANES]
        y = w[0:1] * xc[5:5 + c] + w[1:2] * xc[6:6 + c] + w[2:3] * xc[7:7 + c] + w[3:4] * xc[8:8 + c]
        return _silu(y)

    def body(n, carry):
        r0 = pl.multiple_of(n * c, c)
        sm = sm_ref[0, pl.ds(r0, c), :]
        xg = sm + dt_bias
        softplus = jnp.maximum(xg, 0.0) + jnp.log1p(jnp.exp(-jnp.abs(xg)))
        beta_all = jax.nn.sigmoid(sm)
        gc_all = _hdot(tril, neg_decay_rate * softplus)
        gc_t = gc_all.T
        q, k, v, kb, e_gc, beta, gc, gc_last, lmat, a_intra = ([None] * GDN_HEADS for _ in range(10))
        for h in heads:
            q[h] = conv_silu(q_ref, 0, h, r0, n)
            k[h] = conv_silu(k_ref, 1, h, r0, n)
            v[h] = conv_silu(v_ref, 2, h, r0, n)
            q[h] = q[h] * lax.rsqrt(jnp.sum(q[h] * q[h], axis=-1, keepdims=True) + EPS) * (LANES ** -0.5)
            k[h] = k[h] * lax.rsqrt(jnp.sum(k[h] * k[h], axis=-1, keepdims=True) + EPS)
            beta[h] = beta_all[:, h:h + 1]
            gc[h] = gc_all[:, GDN_HEADS + h:GDN_HEADS + h + 1]
            gc_last[h] = gc_all[c - 1:c, GDN_HEADS + h:GDN_HEADS + h + 1]
            e_gc[h] = jnp.exp(gc[h])
            kb[h] = k[h] * beta[h]
        kq = [_dot_nt(jnp.concatenate([kb[h], q[h]], axis=0).astype(BF16), k[h].astype(BF16)) for h in heads]
        for h in heads:
            gc_row = gc_t[GDN_HEADS + h:GDN_HEADS + h + 1, :]
            decay = jnp.exp(jnp.where(causal, gc[h] - gc_row, NEG_INF))
            lmat[h] = jnp.where(strict, kq[h][:c] * decay, 0.0)
            a_intra[h] = jnp.where(causal, kq[h][c:] * decay, 0.0)
        tmat = unit_lower_inverse(lmat)
        uw = [mm(tmat[h], jnp.concatenate([v[h] * beta[h], kb[h] * e_gc[h]], axis=1)) for h in heads]
        state = [s_scr[h] for h in heads]
        ws = [mm(jnp.concatenate([uw[h][:, LANES:], q[h] * e_gc[h]], axis=0), state[h]) for h in heads]
        v_new = [uw[h][:, :LANES] - ws[h][:c] for h in heads]
        o = [ws[h][c:] + mm(a_intra[h], v_new[h]) for h in heads]
        for h in heads:
            k_d = k[h] * jnp.exp(gc_last[h] - gc[h])
            s_scr[h] = state[h] * jnp.exp(gc_last[h]) + lax.dot_general(
                k_d.astype(BF16), v_new[h].astype(BF16), (((0,), (0,)), ((), ())), preferred_element_type=F32)
        for h in heads:
            y = o[h] * lax.rsqrt(jnp.mean(o[h] * o[h], axis=-1, keepdims=True) + EPS) * norm_w
            o_ref[pl.ds(r0, c), h * LANES:(h + 1) * LANES] = y * _silu(gz_ref[h, pl.ds(r0, c), :])
        return carry

    lax.fori_loop(0, rows // c, body, 0)
    for which, ref in enumerate((q_ref, k_ref, v_ref)):
        for h in range(GDN_HEADS):
            tail_scr[which, h] = ref[h, rows - 8:rows, :]


def _gdn(proj, lay, conv_wt, alog_l, dtb_l, norm_w, *, bsz, seq, rows=512):
    rows = min(rows, seq)
    nsb = seq // rows
    assert lay.gq % GDN_HEADS == 0

    def head_spec(base):
        return pl.BlockSpec((GDN_HEADS, rows, LANES), lambda b, s: (base // GDN_HEADS, b * nsb + s, 0))

    vec = pl.BlockSpec((1, LANES), lambda b, s: (0, 0))
    return pl.pallas_call(
        _gdn_kernel,
        grid=(bsz, nsb),
        in_specs=[head_spec(lay.gq), head_spec(lay.gk), head_spec(lay.gv), head_spec(lay.gz),
                  pl.BlockSpec((1, rows, LANES), lambda b, s: (lay.small, b * nsb + s, 0)),
                  pl.BlockSpec(conv_wt.shape, lambda b, s: (0, 0)), vec, vec, vec],
        out_specs=pl.BlockSpec((rows, GDN_HEADS * LANES), lambda b, s: (b * nsb + s, 0)),
        out_shape=jax.ShapeDtypeStruct((bsz * seq, GDN_HEADS * LANES), F32),
        scratch_shapes=[pltpu.VMEM((GDN_HEADS, LANES, LANES), F32),
                        pltpu.VMEM((3, GDN_HEADS, 8, LANES), F32)],
        compiler_params=_params("parallel", "arbitrary"),
        name="gdn",
    )(proj, proj, proj, proj, proj, conv_wt, alog_l, dtb_l, norm_w)


def _compress_kernel(x_ref, pos_ref, w1_ref, b1_ref, w2_ref, o_ref):
    x = x_ref[0, 0]
    half = x.shape[1]
    pos = pos_ref[0]
    first = _hdot(x + pos[0:1], w1_ref[0, 0:half, :])
    second = _hdot(x + pos[1:2], w1_ref[0, half:2 * half, :])
    rows = x.shape[0]
    hid = first + pltpu.roll(second, rows - 1, 0) + b1_ref[0]
    o_ref[0, 0] = _hdot(_silu(hid), w2_ref[0])


def _compress(proj, lay, pos, w1, b1, w2, *, bsz, seq):
    rows = seq // CMP_STRIDE
    wide = CMP_STRIDE * LANES
    nkv = 2 * NSA_KV_HEADS
    x = proj[lay.nkv:lay.nkv + nkv].reshape(nkv, bsz, rows, wide)
    return pl.pallas_call(
        _compress_kernel,
        grid=(bsz, nkv),
        in_specs=[pl.BlockSpec((1, 1, rows, wide), lambda b, j: (j, b, 0, 0)),
                  pl.BlockSpec((1, 2, wide), lambda b, j: (j // NSA_KV_HEADS, 0, 0)),
                  pl.BlockSpec((1, 2 * wide, LANES), lambda b, j: (j // NSA_KV_HEADS, 0, 0)),
                  pl.BlockSpec((1, 1, LANES), lambda b, j: (j // NSA_KV_HEADS, 0, 0)),
                  pl.BlockSpec((1, LANES, LANES), lambda b, j: (j // NSA_KV_HEADS, 0, 0))],
        out_specs=pl.BlockSpec((1, 1, rows, LANES), lambda b, j: (j, b, 0, 0)),
        out_shape=jax.ShapeDtypeStruct((nkv, bsz, rows, LANES), F32),
        compiler_params=_params("parallel", "arbitrary"),
        name="compress",
    )(x, pos, w1, b1, w2)


def _rope_tables(seq):
    pos = jnp.arange(seq, dtype=F32)
    inv_freq = ROPE_THETA ** (-jnp.arange(0, ROPE_DIM, 2, dtype=F32) / ROPE_DIM)
    ang = pos[:, None] * inv_freq[None, :]
    cos, sin = jnp.cos(ang), jnp.sin(ang)
    half = ROPE_DIM // 2
    rest = LANES - ROPE_DIM
    mult = jnp.concatenate([cos, cos, jnp.ones((seq, rest), F32)], axis=1)
    from_upper = jnp.concatenate([-sin, jnp.zeros((seq, LANES - half), F32)], axis=1)
    from_lower = jnp.concatenate([jnp.zeros((seq, half), F32), sin, jnp.zeros((seq, rest), F32)], axis=1)
    return jnp.stack([mult, from_upper, from_lower])


def _rope(x, tab):
    half = ROPE_DIM // 2
    return (x * tab[0] + pltpu.roll(x, LANES - half, x.ndim - 1) * tab[1]
            + pltpu.roll(x, half, x.ndim - 1) * tab[2])


def _kvprep_kernel(x_ref, tab_ref, o_ref):
    x = x_ref[0]
    is_key = (pl.program_id(0) % 4) < NSA_KV_HEADS
    o_ref[0] = jnp.where(is_key, _rope(x, tab_ref[...]), x).astype(BF16)


def _kvprep(proj, lay, tabs, *, bsz, seq, tr=1024):
    t = bsz * seq
    tr = min(tr, seq)
    per_b = seq // tr
    n = 4 * NSA_KV_HEADS
    return pl.pallas_call(
        _kvprep_kernel,
        grid=(n, t // tr),
        in_specs=[pl.BlockSpec((1, tr, LANES), lambda j, i: (lay.nkv + 2 * NSA_KV_HEADS + j, i, 0)),
                  pl.BlockSpec((3, tr, LANES), lambda j, i: (0, i % per_b, 0))],
        out_specs=pl.BlockSpec((1, tr, LANES), lambda j, i: (j, i, 0)),
        out_shape=jax.ShapeDtypeStruct((n, t, LANES), BF16),
        compiler_params=_params("parallel", "arbitrary"),
        name="kvprep",
    )(proj, tabs)


def _softmax_rows(s, mask):
    s = jnp.where(mask, s, NEG_INF)
    m = jnp.max(s, axis=-1, keepdims=True)
    e = jnp.where(mask, jnp.exp(s - m), 0.0)
    den = jnp.sum(e, axis=-1, keepdims=True)
    return e / jnp.where(den > 0, den, 1.0)


def _nsa_kernel(q_ref, tab_ref, kc_ref, vc_ref, ks_ref, vs_ref, kw_ref, vw_ref, sm_ref, o_ref,
                m_scr, l_scr, acc_scr):
    grp = pl.program_id(1)
    qb = pl.program_id(2)
    nq = Q_BLOCK
    rows = NSA_GROUP * nq
    seq = ks_ref.shape[1]
    n_slc = seq // SLC_BLOCK
    scale = LANES ** -0.5
    q0 = qb * nq

    q_raw = q_ref[...]
    q_rot = _rope(q_raw, tab_ref[...]).reshape(rows, LANES).astype(BF16)
    q_raw = q_raw.reshape(rows, LANES)
    t_row = q0 + lax.broadcasted_iota(jnp.int32, (rows, 1), 0) % nq
    t_tok = q0 + lax.broadcasted_iota(jnp.int32, (nq, 1), 0)

    n_cmp = kc_ref.shape[2]
    cmp_end = lax.broadcasted_iota(jnp.int32, (1, n_cmp), 1) * CMP_STRIDE + (CMP_BLOCK - 1)
    p_c = _softmax_rows(_hdot_nt(q_raw, kc_ref[0, 0]) * scale, cmp_end <= t_row)
    o_c = _hdot(p_c, vc_ref[0, 0])

    imp_cmp = p_c[0:nq]
    for r in range(1, NSA_GROUP):
        imp_cmp = imp_cmp + p_c[r * nq:(r + 1) * nq]
    per_slc = SLC_BLOCK // CMP_STRIDE
    pool = (lax.broadcasted_iota(jnp.int32, (n_cmp, n_slc), 0) // per_slc
            == lax.broadcasted_iota(jnp.int32, (n_cmp, n_slc), 1)).astype(F32)
    imp = _hdot(imp_cmp, pool)
    blk = lax.broadcasted_iota(jnp.int32, (1, n_slc), 1)
    t_blk = t_tok // SLC_BLOCK
    visible = blk * SLC_BLOCK <= t_tok
    forced = (blk == 0) | (blk == t_blk) | (blk == t_blk - 1)
    score = jnp.where(visible, imp + jnp.where(forced, FORCE_BONUS, 0.0), NEG_INF)
    rank = jnp.zeros((nq, n_slc), jnp.int32)
    for i in range(n_slc):
        s_i = score[:, i:i + 1]
        ahead = (s_i > score) | ((s_i == score) & (blk > i))
        rank = rank + ahead.astype(jnp.int32)
    sel = ((rank < min(SLC_TOPK, n_slc)) & visible).astype(BF16)
    sel_rows = jnp.concatenate([sel] * NSA_GROUP, axis=0)

    tk = min(SEL_KEY_TILE, seq)
    m_scr[...] = jnp.full_like(m_scr, NEG_INF)
    l_scr[...] = jnp.zeros_like(l_scr)
    acc_scr[...] = jnp.zeros_like(acc_scr)

    def sel_step(kt, carry):
        k0 = pl.multiple_of(kt * tk, tk)
        k = ks_ref[0, pl.ds(k0, tk), :]
        v = vs_ref[0, pl.ds(k0, tk), :]
        s = _dot_nt(q_rot, k) * scale
        kpos = k0 + lax.broadcasted_iota(jnp.int32, (1, tk), 1)
        expand = ((k0 + lax.broadcasted_iota(jnp.int32, (n_slc, tk), 1)) // SLC_BLOCK
                  == lax.broadcasted_iota(jnp.int32, (n_slc, tk), 0)).astype(BF16)
        mask = (_dot(sel_rows, expand) > 0.5) & (kpos <= t_row)
        s = jnp.where(mask, s, NEG_INF)
        m_old = m_scr[...]
        m_new = jnp.maximum(m_old, jnp.max(s, axis=-1, keepdims=True))
        alpha = jnp.exp(m_old - m_new)
        p = jnp.where(mask, jnp.exp(s - m_new), 0.0)
        l_scr[...] = alpha * l_scr[...] + jnp.sum(p, axis=-1, keepdims=True)
        acc_scr[...] = alpha * acc_scr[...] + _dot(p.astype(BF16), v)
        m_scr[...] = m_new
        return carry

    lax.fori_loop(0, (q0 + nq + tk - 1) // tk, sel_step, 0)
    den = l_scr[...]
    o_s = acc_scr[...] / jnp.where(den > 0, den, 1.0)

    span = min(WINDOW + nq, seq)
    w0 = pl.multiple_of(jnp.maximum(q0 + nq - span, 0), nq)
    kpos = w0 + lax.broadcasted_iota(jnp.int32, (1, span), 1)
    dist = t_row - kpos
    p_w = _softmax_rows(_dot_nt(q_rot, kw_ref[0, pl.ds(w0, span), :]) * scale, (dist >= 0) & (dist < WINDOW))
    o_w = _dot(p_w.astype(BF16), vw_ref[0, pl.ds(w0, span), :])

    gates = jax.nn.sigmoid(sm_ref[0])
    lane = lax.broadcasted_iota(jnp.int32, (nq, LANES), 1)
    base = 2 * GDN_HEADS + grp * (3 * NSA_GROUP)
    for r in range(NSA_GROUP):
        out = jnp.zeros((nq, LANES), F32)
        for j, branch in enumerate((o_c, o_s, o_w)):
            gate = jnp.sum(jnp.where(lane == base + 3 * r + j, gates, 0.0), axis=1, keepdims=True)
            out = out + gate * branch[r * nq:(r + 1) * nq]
        o_ref[:, r * LANES:(r + 1) * LANES] = out


def _nsa(proj, lay, tabs, kvcmp, kvatt, *, bsz, seq):
    nq = Q_BLOCK
    nqb = seq // nq
    rows = NSA_GROUP * nq
    n_cmp = kvcmp.shape[2]

    def cmp_spec(base):
        return pl.BlockSpec((1, 1, n_cmp, LANES), lambda b, g, i: (base + g, b, 0, 0))

    def kv_spec(base):
        return pl.BlockSpec((1, seq, LANES), lambda b, g, i: (base + g, b, 0))

    return pl.pallas_call(
        _nsa_kernel,
        grid=(bsz, NSA_KV_HEADS, nqb),
        in_specs=[pl.BlockSpec((NSA_GROUP, nq, LANES), lambda b, g, i: (lay.nq // NSA_GROUP + g, b * nqb + i, 0)),
                  pl.BlockSpec((3, nq, LANES), lambda b, g, i: (0, i, 0)),
                  cmp_spec(0), cmp_spec(NSA_KV_HEADS),
                  kv_spec(0), kv_spec(NSA_KV_HEADS), kv_spec(2 * NSA_KV_HEADS), kv_spec(3 * NSA_KV_HEADS),
                  pl.BlockSpec((1, nq, LANES), lambda b, g, i: (lay.small, b * nqb + i, 0))],
        out_specs=pl.BlockSpec((nq, NSA_GROUP * LANES), lambda b, g, i: (b * nqb + i, g)),
        out_shape=jax.ShapeDtypeStruct((bsz * seq, NSA_HEADS * LANES), F32),
        scratch_shapes=[pltpu.VMEM((rows, 1), F32), pltpu.VMEM((rows, 1), F32), pltpu.VMEM((rows, LANES), F32)],
        compiler_params=_params("parallel", "parallel", "arbitrary"),
        name="nsa",
    )(proj, tabs, kvcmp, kvcmp, kvatt, kvatt, kvatt, kvatt, proj)


def _outproj_kernel(oa_ref, ob_ref, ma_ref, mb_ref, x_ref, mod_ref, wa_ref, wb_ref, wo_ref, o_ref, mrg_scr):
    ya = _dot(oa_ref[...].astype(BF16), wa_ref[...])
    yb = _dot(ob_ref[...].astype(BF16), wb_ref[...])
    for c in range(ma_ref.shape[0]):
        sl = slice(c * LANES, (c + 1) * LANES)
        merged = jax.nn.sigmoid(ma_ref[c]) * ya[:, sl] + jax.nn.sigmoid(mb_ref[c]) * yb[:, sl]
        mrg_scr[:, sl] = merged.astype(BF16)
    o_ref[...] = x_ref[...] + mod_ref[0][5:6] * _dot(mrg_scr[...], wo_ref[...])


def _outproj(o_a, o_b, proj, lay, x, mods, wa, wb, wo, *, seq, tm=256):
    t, d = x.shape
    half = d // LANES
    tm = min(tm, seq)
    per_b = seq // tm

    def resident(shape):
        return pl.BlockSpec(shape, lambda i: (0, 0), pipeline_mode=pl.Buffered(1))

    return pl.pallas_call(
        _outproj_kernel,
        grid=(t // tm,),
        in_specs=[pl.BlockSpec((tm, o_a.shape[1]), lambda i: (i, 0)),
                  pl.BlockSpec((tm, o_b.shape[1]), lambda i: (i, 0)),
                  pl.BlockSpec((half, tm, LANES), lambda i: (lay.ma // half, i, 0)),
                  pl.BlockSpec((half, tm, LANES), lambda i: (lay.mb // half, i, 0)),
                  pl.BlockSpec((tm, d), lambda i: (i, 0)),
                  pl.BlockSpec((1, N_MOD, d), lambda i: (i // per_b, 0, 0)),
                  resident(wa.shape), resident(wb.shape), resident(wo.shape)],
        out_specs=pl.BlockSpec((tm, d), lambda i: (i, 0)),
        out_shape=jax.ShapeDtypeStruct((t, d), F32),
        scratch_shapes=[pltpu.VMEM((tm, d), BF16)],
        compiler_params=_params("parallel"),
        name="outproj",
    )(o_a, o_b, proj, proj, x, mods, wa, wb, wo)


def _rearranged_w_in(w_in, lay, d):
    gw = GDN_HEADS * LANES
    nw = NSA_HEADS * LANES
    kvw = 6 * NSA_KV_HEADS * LANES
    o_gb = 3 * gw
    o_gz = o_gb + 2 * GDN_HEADS
    o_nq = o_gz + gw
    o_nkv = o_nq + nw
    o_ng = o_nkv + kvw
    o_mg = o_ng + 3 * NSA_HEADS
    small = jnp.concatenate([w_in[:, o_gb:o_gz], w_in[:, o_ng:o_mg]], axis=1)
    pad = lay.nblk * LANES - (2 * d + 3 * gw + gw + nw + kvw + small.shape[1])
    cols = [w_in[:, o_mg:o_mg + 2 * d], w_in[:, 0:o_gb], w_in[:, o_gz:o_nq], w_in[:, o_nq:o_nkv],
            w_in[:, o_nkv:o_ng], small, jnp.zeros((d, pad), w_in.dtype)]
    return jnp.concatenate(cols, axis=1).astype(BF16)


def _lane_vec(v, offset):
    return jnp.zeros((1, LANES), F32).at[0, offset:offset + v.shape[0]].set(v.astype(F32))


def kernel(x, c, ada_w, ada_b, ffn1_norm, ffn1_w_gate, ffn1_w_up, ffn1_w_down, mix_norm, w_in, gdn_conv_w, gdn_a_log, gdn_dt_bias, gdn_norm_w, gdn_w_up, cmp_pos_k, cmp_k_w1, cmp_k_b1, cmp_k_w2, cmp_pos_v, cmp_v_w1, cmp_v_b1, cmp_v_w2, nsa_w_up, w_out, ffn2_norm, ffn2_w_gate, ffn2_w_up, ffn2_w_down, final_norm):
    bsz, seq, d = x.shape
    depth = ada_w.shape[0]
    lay = _Layout(d)
    tabs = _rope_tables(seq)
    xf = x.reshape(bsz * seq, d)
    for l in range(depth):
        last = l == depth - 1
        mods = _ada(c, ada_w[l], ada_b[l]).reshape(bsz, N_MOD, d)
        xf = _ffn(xf, mods, ffn1_norm[l], ffn1_w_gate[l].astype(BF16), ffn1_w_up[l].astype(BF16),
                  ffn1_w_down[l].astype(BF16), final_norm, mod_base=0, final=False, seq=seq)
        proj = _inproj(xf, mods, mix_norm[l], _rearranged_w_in(w_in[l], lay, d), seq=seq)
        o_a = _gdn(proj, lay, gdn_conv_w[l].T, _lane_vec(gdn_a_log[l], GDN_HEADS),
                   _lane_vec(gdn_dt_bias[l], GDN_HEADS), gdn_norm_w[l].reshape(1, LANES), bsz=bsz, seq=seq)
        wide = CMP_STRIDE * LANES
        kvcmp = _compress(proj, lay,
                          jnp.stack([cmp_pos_k[l], cmp_pos_v[l]]).reshape(2, 2, wide),
                          jnp.stack([cmp_k_w1[l], cmp_v_w1[l]]),
                          jnp.stack([cmp_k_b1[l], cmp_v_b1[l]]).reshape(2, 1, LANES),
                          jnp.stack([cmp_k_w2[l], cmp_v_w2[l]]), bsz=bsz, seq=seq)
        kvatt = _kvprep(proj, lay, tabs, bsz=bsz, seq=seq)
        o_b = _nsa(proj, lay, tabs, kvcmp, kvatt, bsz=bsz, seq=seq)
        xf = _outproj(o_a, o_b, proj, lay, xf, mods, gdn_w_up[l].astype(BF16), nsa_w_up[l].astype(BF16),
                      w_out[l].astype(BF16), seq=seq)
        xf = _ffn(xf, mods, ffn2_norm[l], ffn2_w_gate[l].astype(BF16), ffn2_w_up[l].astype(BF16),
                  ffn2_w_down[l].astype(BF16), final_norm, mod_base=6, final=last, seq=seq)
    return xf.reshape(bsz, seq, d)
```
